```python
import math, functools
import jax, jax.numpy as jnp
from jax import lax
import numpy as np

D_MODEL = 1024
BATCH = 2
SEQ = 8192
DEPTH = 2
DEC_BATCH = 128
DEC_SEQ = 4
PAST_LEN = 2048
PAGE_SIZE = 128

H_A = 4
DH_A = 64
DV_A = 2 * DH_A
ATTN_SCALE = DH_A ** -0.5
Q_BLOCK = 128
H_B = 4
DK_B = 64
DV_B = 128
RET_CHUNK = 128
ROPE_THETA = 10000.0
D_FF = 2816
N_EXPERTS = 8
TOP_K = 2
D_FF_E = 2816
N_DENSE = (DEPTH + 1) // 2
N_MOE = DEPTH // 2
N_MOD = 6
EPS = 1e-6
A_QK = H_A * 2 * DH_A
A_V = H_A * DV_A
B_QK = H_B * DK_B
B_V = H_B * DV_B
SPLITS = (A_QK, A_QK, A_V, B_QK, B_QK, B_V, B_V, D_MODEL, D_MODEL)
IN_W = 2 * A_QK + A_V + 2 * B_QK + 2 * B_V + 2 * D_MODEL

kernel_name = 'hybrid_diffattn_retention_adaln_decode_step'


def rms_norm(x, gain=None):
    xf = x.astype(jnp.float32)
    y = xf * lax.rsqrt(jnp.mean(xf * xf, axis=-1, keepdims=True) + EPS)
    if gain is not None:
        y = y * gain.astype(jnp.float32)
    return y.astype(x.dtype)


def rope(x, pos):
    d = x.shape[-1]
    inv = ROPE_THETA ** (-jnp.arange(0, d, 2, dtype=jnp.float32) / d)
    ang = pos.astype(jnp.float32)[:, None] * inv[None, :]
    ang = jnp.concatenate([ang, ang], axis=-1)
    shape = (pos.shape[0],) + (1,) * (x.ndim - 3) + (d,)
    cos = jnp.cos(ang).reshape(shape)
    sin = jnp.sin(ang).reshape(shape)
    xf = x.astype(jnp.float32)
    rot = jnp.concatenate([-xf[..., d // 2:], xf[..., :d // 2]], axis=-1)
    return (xf * cos + rot * sin).astype(x.dtype)


def modulation(c, w, b):
    m = jax.nn.silu(c) @ w + b
    return [t[:, None, :] for t in jnp.split(m, N_MOD, axis=-1)]


def mixer_inputs(h, pos, w_in, q_gain, k_gain):
    B, S, _ = h.shape
    idx = [int(i) for i in np.cumsum(SPLITS)[:-1]]
    qa, ka, va, qb, kb, vb, g_ret, g_a, g_b = jnp.split(h @ w_in, idx, axis=-1)
    qa = rope(rms_norm(qa.reshape(B, S, H_A, 2, DH_A), q_gain), pos)
    ka = rope(rms_norm(ka.reshape(B, S, H_A, 2, DH_A), k_gain), pos)
    va = va.reshape(B, S, H_A, DV_A)
    qb = rope(qb.reshape(B, S, H_B, DK_B), pos)
    kb = rope(kb.reshape(B, S, H_B, DK_B), pos) * (DK_B ** -0.5)
    vb = vb.reshape(B, S, H_B, DV_B)
    return qa, ka, va, qb, kb, vb, jax.nn.silu(g_ret), jax.nn.sigmoid(g_a), jax.nn.sigmoid(g_b)


def diff_lambda(l, lq1, lk1, lq2, lk2):
    lam_init = 0.8 - 0.6 * math.exp(-0.3 * l)
    lam = (jnp.exp(jnp.sum(lq1.astype(jnp.float32) * lk1.astype(jnp.float32)))
           - jnp.exp(jnp.sum(lq2.astype(jnp.float32) * lk2.astype(jnp.float32))) + lam_init)
    return lam, lam_init


def diff_weights(s, lam):
    p = jax.nn.softmax(s, axis=-1)
    return p[:, :, 0] - lam * p[:, :, 1]


def diff_attn_prompt(qa, ka, va, lam):
    B, S = qa.shape[:2]
    nb = S // Q_BLOCK
    q_blocks = qa.reshape(B, nb, Q_BLOCK, H_A, 2, DH_A).swapaxes(0, 1)
    starts = jnp.arange(nb) * Q_BLOCK
    k_pos = jnp.arange(S)

    def one_block(args):
        q_blk, start = args
        s = jnp.einsum('bqhcd,bkhcd->bhcqk', q_blk, ka).astype(jnp.float32) * ATTN_SCALE
        q_pos = start + jnp.arange(Q_BLOCK)
        s = jnp.where(k_pos[None, :] <= q_pos[:, None], s, -jnp.inf)
        a = diff_weights(s, lam).astype(va.dtype)
        return jnp.einsum('bhqk,bkhv->bqhv', a, va)

    o = lax.map(one_block, (q_blocks, starts))
    return o.swapaxes(0, 1).reshape(B, S, H_A, DV_A)


def diff_attn_sample(qa, ka, va, lam, k_past, v_past):
    T = qa.shape[1]
    P = k_past.shape[1]
    s_past = jnp.einsum('bqhcd,bkhcd->bhcqk', qa, k_past).astype(jnp.float32) * ATTN_SCALE
    s_new = jnp.einsum('bqhcd,bkhcd->bhcqk', qa, ka).astype(jnp.float32) * ATTN_SCALE
    causal = jnp.tril(jnp.ones((T, T), dtype=bool))
    s = jnp.concatenate([s_past, jnp.where(causal, s_new, -jnp.inf)], axis=-1)
    a = diff_weights(s, lam).astype(va.dtype)
    return (jnp.einsum('bhqk,bkhv->bqhv', a[..., :P], v_past)
            + jnp.einsum('bhqk,bkhv->bqhv', a[..., P:], va))


def retention_log_decay():
    return jnp.log(1.0 - 2.0 ** (-5.0 - jnp.arange(H_B, dtype=jnp.float32)))


def retention_chunk(q, k, v, state, log_gamma):
    q, k, v = (t.astype(jnp.float32) for t in (q, k, v))
    state = state.astype(jnp.float32)
    C = q.shape[1]
    i = jnp.arange(C, dtype=jnp.float32)
    rel = i[:, None] - i[None, :]
    decay = jnp.where(rel >= 0, jnp.exp(log_gamma[:, None, None] * jnp.maximum(rel, 0.0)), 0.0)
    scores = jnp.einsum('bihd,bjhd->bhij', q, k) * decay
    q_fac = jnp.exp((i[:, None] + 1.0) * log_gamma[None, :])[None, :, :, None]
    k_fac = jnp.exp((C - 1.0 - i)[:, None] * log_gamma[None, :])[None, :, :, None]
    o = (jnp.einsum('bhij,bjhv->bihv', scores, v)
         + jnp.einsum('bihd,bhdv->bihv', q * q_fac, state))
    new_state = (jnp.exp(C * log_gamma)[None, :, None, None] * state
                 + jnp.einsum('bjhd,bjhv->bhdv', k * k_fac, v))
    return o, new_state


def retention_prompt(q, k, v, log_gamma):
    B, S = q.shape[:2]
    nc = S // RET_CHUNK

    def to_chunks(t):
        return t.astype(jnp.float32).reshape((B, nc, RET_CHUNK) + t.shape[2:]).swapaxes(0, 1)

    def step(state, xs):
        o, state = retention_chunk(xs[0], xs[1], xs[2], state, log_gamma)
        return state, o

    s0 = jnp.zeros((B, H_B, DK_B, DV_B), jnp.float32)
    s_fin, o = lax.scan(step, s0, (to_chunks(q), to_chunks(k), to_chunks(v)))
    return o.swapaxes(0, 1).reshape(B, S, H_B, DV_B), s_fin


def swiglu(h, wg, wu, wd):
    return (jax.nn.silu(h @ wg) * (h @ wu)) @ wd


def moe_ffn(h, w_r, b_r, wg, wu, wd):
    logits = (h @ w_r).astype(jnp.float32) + b_r.astype(jnp.float32)
    top_v, top_i = lax.top_k(logits, TOP_K)
    top_w = jax.nn.softmax(top_v, axis=-1)
    gates = jnp.sum(jax.nn.one_hot(top_i, N_EXPERTS, dtype=jnp.float32) * top_w[..., None], axis=-2).astype(h.dtype)
    y = jnp.zeros_like(h)
    for e in range(N_EXPERTS):
        y = y + gates[..., e:e + 1] * swiglu(h, wg[e], wu[e], wd[e])
    return y


def setup_inputs(seed: int = 0) -> dict:
    key = jax.random.key(seed)
    ks = jax.random.split(key, 32)

    def nrm(i, shape, scale):
        return jax.random.normal(ks[i], shape, jnp.float32) * scale

    n_pages = PAST_LEN // PAGE_SIZE
    n_used = DEC_BATCH * n_pages
    n_pool = n_used + max(1, n_used // 4)
    page_table = jax.random.permutation(ks[0], n_pool)[:n_used].reshape(DEC_BATCH, n_pages).astype(jnp.int32)
    return {
        'x_prompt': nrm(1, (BATCH, SEQ, D_MODEL), 1.0),
        'x_sample': nrm(2, (DEC_BATCH, DEC_SEQ, D_MODEL), 1.0),
        'cache_k': nrm(3, (DEPTH, n_pool, PAGE_SIZE, H_A, 2, DH_A), 1.0),
        'cache_v': nrm(4, (DEPTH, n_pool, PAGE_SIZE, H_A, DV_A), 1.0),
        'state_ret': nrm(5, (DEPTH, DEC_BATCH, H_B, DK_B, DV_B), 0.5),
        'page_table': page_table,
        'c_prompt': nrm(6, (BATCH, D_MODEL), 1.0),
        'c_sample': nrm(7, (DEC_BATCH, D_MODEL), 1.0),
        'w_mod': nrm(8, (DEPTH, D_MODEL, N_MOD * D_MODEL), 0.5 * D_MODEL ** -0.5),
        'b_mod': nrm(9, (DEPTH, N_MOD * D_MODEL), 0.01),
        'norm_mix': 1.0 + nrm(10, (DEPTH, D_MODEL), 0.02),
        'norm_ffn': 1.0 + nrm(11, (DEPTH, D_MODEL), 0.02),
        'w_in': nrm(12, (DEPTH, D_MODEL, IN_W), D_MODEL ** -0.5),
        'q_norm': 1.0 + nrm(13, (DEPTH, DH_A), 0.02),
        'k_norm': 1.0 + nrm(14, (DEPTH, DH_A), 0.02),
        'lambda_q1': nrm(15, (DEPTH, DH_A), 0.1),
        'lambda_k1': nrm(16, (DEPTH, DH_A), 0.1),
        'lambda_q2': nrm(17, (DEPTH, DH_A), 0.1),
        'lambda_k2': nrm(18, (DEPTH, DH_A), 0.1),
        'sub_norm': 1.0 + nrm(19, (DEPTH, DV_A), 0.02),
        'w_o_attn': nrm(20, (DEPTH, A_V, D_MODEL), A_V ** -0.5),
        'w_o_ret': nrm(21, (DEPTH, B_V, D_MODEL), B_V ** -0.5),
        'w_out': nrm(22, (DEPTH, D_MODEL, D_MODEL), D_MODEL ** -0.5),
        'ffn_w_gate': nrm(23, (N_DENSE, D_MODEL, D_FF), D_MODEL ** -0.5),
        'ffn_w_up': nrm(24, (N_DENSE, D_MODEL, D_FF), D_MODEL ** -0.5),
        'ffn_w_down': nrm(25, (N_DENSE, D_FF, D_MODEL), D_FF ** -0.5),
        'w_router': nrm(26, (N_MOE, D_MODEL, N_EXPERTS), D_MODEL ** -0.5),
        'b_router': nrm(27, (N_MOE, N_EXPERTS), 0.01),
        'moe_w_gate': nrm(28, (N_MOE, N_EXPERTS, D_MODEL, D_FF_E), D_MODEL ** -0.5),
        'moe_w_up': nrm(29, (N_MOE, N_EXPERTS, D_MODEL, D_FF_E), D_MODEL ** -0.5),
        'moe_w_down': nrm(30, (N_MOE, N_EXPERTS, D_FF_E, D_MODEL), D_FF_E ** -0.5),
    }


def reference(x_prompt, x_sample, cache_k, cache_v, state_ret, page_table, c_prompt, c_sample,
              w_mod, b_mod, norm_mix, norm_ffn, w_in, q_norm, k_norm,
              lambda_q1, lambda_k1, lambda_q2, lambda_k2, sub_norm,
              w_o_attn, w_o_ret, w_out, ffn_w_gate, ffn_w_up, ffn_w_down,
              w_router, b_router, moe_w_gate, moe_w_up, moe_w_down):
    n_pages = page_table.shape[1]
    dec_b = page_table.shape[0]
    past = n_pages * PAGE_SIZE
    pos_p = jnp.arange(x_prompt.shape[1])
    pos_s = past + jnp.arange(x_sample.shape[1])
    log_gamma = retention_log_decay()

    def run_layer(x, c, pos, l, attend, retain):
        B, S, _ = x.shape
        sh1, sc1, g1, sh2, sc2, g2 = modulation(c, w_mod[l], b_mod[l])
        h = rms_norm(x, norm_mix[l]) * (1.0 + sc1) + sh1
        qa, ka, va, qb, kb, vb, g_ret, g_a, g_b = mixer_inputs(h, pos, w_in[l], q_norm[l], k_norm[l])
        lam, lam_init = diff_lambda(l, lambda_q1[l], lambda_k1[l], lambda_q2[l], lambda_k2[l])
        oa = rms_norm(attend(qa, ka, va, lam), sub_norm[l]) * (1.0 - lam_init)
        out_a = oa.reshape(B, S, A_V) @ w_o_attn[l]
        ob, s_new = retain(qb, kb, vb)
        ob = rms_norm(ob).astype(x.dtype).reshape(B, S, B_V)
        out_b = (ob * g_ret) @ w_o_ret[l]
        mix = g_a * out_a + g_b * out_b
        x = x + g1 * (mix @ w_out[l])
        h2 = rms_norm(x, norm_ffn[l]) * (1.0 + sc2) + sh2
        if l % 2 == 0:
            j = l // 2
            f = swiglu(h2, ffn_w_gate[j], ffn_w_up[j], ffn_w_down[j])
        else:
            j = l // 2
            f = moe_ffn(h2, w_router[j], b_router[j], moe_w_gate[j], moe_w_up[j], moe_w_down[j])
        x = x + g2 * f
        return x, ka, va, s_new

    xp, xs = x_prompt, x_sample
    kp_l, vp_l, sp_l, ks_l, vs_l, ss_l = [], [], [], [], [], []
    for l in range(DEPTH):
        xp, kp, vp, sp = run_layer(xp, c_prompt, pos_p, l, diff_attn_prompt,
                                   functools.partial(retention_prompt, log_gamma=log_gamma))
        k_past = cache_k[l][page_table].reshape(dec_b, past, H_A, 2, DH_A)
        v_past = cache_v[l][page_table].reshape(dec_b, past, H_A, DV_A)
        xs, kn, vn, sn = run_layer(xs, c_sample, pos_s, l,
                                   functools.partial(diff_attn_sample, k_past=k_past, v_past=v_past),
                                   functools.partial(retention_chunk, state=state_ret[l], log_gamma=log_gamma))
        kp_l.append(kp); vp_l.append(vp); sp_l.append(sp)
        ks_l.append(kn); vs_l.append(vn); ss_l.append(sn)

    k_prompt = jnp.stack(kp_l)
    v_prompt = jnp.stack(vp_l)
    ret_prompt = jnp.stack(sp_l)
    k_sample = jnp.stack(ks_l)
    v_sample = jnp.stack(vs_l)
    ret_sample = jnp.stack(ss_l)
    return (xp, xs, k_prompt, v_prompt, ret_prompt, k_sample, v_sample, ret_sample)
```

```python
import functools
import math

import jax
import jax.numpy as jnp
from jax import lax
from jax.experimental import pallas as pl
from jax.experimental.pallas import tpu as pltpu

F32 = jnp.float32
BF16 = jnp.bfloat16

D_MODEL = 1024
H_A = 4
DH_A = 64
DV_A = 128
H_B = 4
DK_B = 64
DV_B = 128
A_QK = H_A * 2 * DH_A
A_V = H_A * DV_A
B_QK = H_B * DK_B
B_V = H_B * DV_B
IN_W = 2 * A_QK + A_V + 2 * B_QK + 2 * B_V + 2 * D_MODEL
N_MOD = 6
N_EXPERTS = 8
PAGE_SIZE = 128
RET_CHUNK = 128
ROPE_THETA = 10000.0
EPS = 1e-6
ATTN_SCALE = DH_A ** -0.5
RET_K_SCALE = DK_B ** -0.5

C_QA = 0
C_KA = C_QA + A_QK
C_VA = C_KA + A_QK
C_QB = C_VA + A_V
C_KB = C_QB + B_QK
C_VB = C_KB + B_QK
C_GR = C_VB + B_V
C_GA = C_GR + B_V
C_GB = C_GA + D_MODEL

LANES = 128
SUBLANES = 8
ROW_TILE = 512
EXPERT_TILE = 512
ATTN_TILE = 512
VMEM_LIMIT = 56 * 1024 * 1024

NT_DIMS = (((1,), (1,)), ((), ()))


def _params(sem, vmem=VMEM_LIMIT):
    return pltpu.CompilerParams(dimension_semantics=sem, vmem_limit_bytes=vmem)


def _rms(x):
    return x * lax.rsqrt(jnp.mean(x * x, axis=-1, keepdims=True) + EPS)


def _silu(x):
    return x * jax.nn.sigmoid(x)


def _diff_lambda(lq1, lk1, lq2, lk2, lam_init):
    a = jnp.sum(lq1[...] * lk1[...], axis=-1, keepdims=True)
    b = jnp.sum(lq2[...] * lk2[...], axis=-1, keepdims=True)
    return jnp.exp(a) - jnp.exp(b) + lam_init


def _lam_init(layer):
    return 0.8 - 0.6 * math.exp(-0.3 * layer)


def _log_gamma(h):
    return math.log(1.0 - 2.0 ** (-5.0 - h))


def _mod_kernel(c_ref, w_ref, b_ref, o_ref):
    a = _silu(c_ref[...]).astype(BF16)
    o_ref[...] = jnp.dot(a, w_ref[...].astype(BF16), preferred_element_type=F32) + b_ref[...]


def _modulation(c_all, w_mod, b_mod):
    depth, d, n = w_mod.shape
    m = c_all.shape[0]
    tn = 1024
    return pl.pallas_call(
        _mod_kernel,
        grid=(depth, n // tn),
        in_specs=[
            pl.BlockSpec((m, d), lambda l, j: (0, 0)),
            pl.BlockSpec((None, d, tn), lambda l, j: (l, 0, j)),
            pl.BlockSpec((None, 1, tn), lambda l, j: (l, 0, j)),
        ],
        out_specs=pl.BlockSpec((None, m, tn), lambda l, j: (l, 0, j)),
        out_shape=jax.ShapeDtypeStruct((depth, m, n), F32),
        compiler_params=_params(("arbitrary", "arbitrary")),
        name="modulation",
    )(c_all, w_mod, b_mod.reshape(depth, 1, n))


def _proj_in_kernel(x_ref, sc_ref, sh_ref, gain_ref, w_ref, qg_ref, kg_ref, cos_ref, sin_ref,
                    q_ref, k32_ref, k16_ref, v32_ref, v16_ref, qb_ref, kb_ref, vb_ref,
                    gr_ref, ga_ref, gb_ref):
    x = x_ref[...]
    h = (_rms(x) * gain_ref[...] * (1.0 + sc_ref[...]) + sh_ref[...]).astype(BF16)

    lane = lax.broadcasted_iota(jnp.int32, (1, LANES), 1)
    first_half = (lane % DH_A) < (DH_A // 2)
    gr = lax.broadcasted_iota(jnp.int32, (LANES, LANES), 0) // DH_A
    gc = lax.broadcasted_iota(jnp.int32, (LANES, LANES), 1) // DH_A
    group_ones = (gr == gc).astype(BF16)
    cos = cos_ref[...]
    sin = sin_ref[...]

    def rope(y):
        rot = jnp.where(first_half, pltpu.roll(y, LANES - DH_A // 2, 1), pltpu.roll(y, DH_A // 2, 1))
        return y * cos + rot * sin

    def group_norm(y, g):
        ss = jnp.dot((y * y).astype(BF16), group_ones, preferred_element_type=F32)
        return y * lax.rsqrt(ss * (1.0 / DH_A) + EPS) * g

    def proj(c0, width):
        return jnp.dot(h, w_ref[:, c0:c0 + width], preferred_element_type=F32)

    def slabs(width):
        return [slice(j * LANES, (j + 1) * LANES) for j in range(width // LANES)]

    y = proj(C_QA, A_QK)
    for s in slabs(A_QK):
        q_ref[:, s] = (rope(group_norm(y[:, s], qg_ref[...])) * ATTN_SCALE).astype(BF16)
    y = proj(C_KA, A_QK)
    for s in slabs(A_QK):
        kk = rope(group_norm(y[:, s], kg_ref[...]))
        k32_ref[:, s] = kk
        k16_ref[:, s] = kk.astype(BF16)
    y = proj(C_VA, A_V)
    v32_ref[...] = y
    v16_ref[...] = y.astype(BF16)
    y = proj(C_QB, B_QK)
    for s in slabs(B_QK):
        qb_ref[:, s] = rope(y[:, s]).astype(BF16)
    y = proj(C_KB, B_QK)
    for s in slabs(B_QK):
        kb_ref[:, s] = (rope(y[:, s]) * RET_K_SCALE).astype(BF16)
    vb_ref[...] = proj(C_VB, B_V).astype(BF16)
    gr_ref[...] = _silu(proj(C_GR, B_V)).astype(BF16)
    ga_ref[...] = jax.nn.sigmoid(proj(C_GA, D_MODEL)).astype(BF16)
    gb_ref[...] = jax.nn.sigmoid(proj(C_GB, D_MODEL)).astype(BF16)


def _mod_spec(arr, idx):
    return pl.BlockSpec((None,) + arr.shape[1:], idx)


def _proj_in(x, mods, gain, w_in16, q_gain, k_gain, cos, sin, pos_blocks, tm):
    n, d = x.shape
    sh1, sc1 = mods["arr"][0], mods["arr"][1]
    idx = mods["idx"](tm)
    row = lambda w: pl.BlockSpec((tm, w), lambda i: (i, 0))
    const = lambda a: pl.BlockSpec(a.shape, lambda i: (0,) * a.ndim)
    pos = pl.BlockSpec((tm, LANES), lambda i: (i % pos_blocks, 0))
    outs = [(A_QK, BF16), (A_QK, F32), (A_QK, BF16), (A_V, F32), (A_V, BF16), (B_QK, BF16),
            (B_QK, BF16), (B_V, BF16), (B_V, BF16), (D_MODEL, BF16), (D_MODEL, BF16)]
    return pl.pallas_call(
        _proj_in_kernel,
        grid=(n // tm,),
        in_specs=[row(d), _mod_spec(sc1, idx), _mod_spec(sh1, idx), const(gain),
                  pl.BlockSpec(w_in16.shape, lambda i: (0, 0), pipeline_mode=pl.Buffered(1)),
                  const(q_gain), const(k_gain), pos, pos],
        out_specs=[row(w) for w, _ in outs],
        out_shape=[jax.ShapeDtypeStruct((n, w), dt) for w, dt in outs],
        compiler_params=_params(("arbitrary",)),
        name="proj_in",
    )(x, sc1, sh1, gain, w_in16, q_gain, k_gain, cos, sin)


def _attn_prompt_kernel(qi_tab, ki_tab, q_ref, k_ref, v_ref, lq1, lk1, lq2, lk2, sg_ref, o_ref,
                        m_scr, l_scr, acc_scr, *, lam_init, tq, tk):
    p = pl.program_id(2)
    qi = qi_tab[p]
    ki = ki_tab[p]
    kpq = tq // tk

    @pl.when(ki == 0)
    def _():
        m_scr[...] = jnp.full(m_scr.shape, -jnp.inf, F32)
        l_scr[...] = jnp.zeros(l_scr.shape, F32)
        acc_scr[...] = jnp.zeros(acc_scr.shape, F32)

    def step(masked):
        q = q_ref[...]
        k = k_ref[...]
        v = v_ref[...]
        lane = lax.broadcasted_iota(jnp.int32, (1, LANES), 1)
        if masked:
            row = qi * tq + lax.broadcasted_iota(jnp.int32, (tq, tk), 0)
            col = ki * tk + lax.broadcasted_iota(jnp.int32, (tq, tk), 1)
            keep = col <= row
        for c in range(2):
            in_map = (lane < DH_A) if c == 0 else (lane >= DH_A)
            qc = jnp.where(in_map, q, jnp.zeros_like(q))
            s = lax.dot_general(qc, k, NT_DIMS, preferred_element_type=F32)
            if masked:
                s = jnp.where(keep, s, -jnp.inf)
            m_prev = m_scr[c]
            m_new = jnp.maximum(m_prev, jnp.max(s, axis=-1, keepdims=True))
            alpha = jnp.exp(m_prev - m_new)
            pr = jnp.exp(s - m_new)
            l_scr[c] = alpha * l_scr[c] + jnp.sum(pr, axis=-1, keepdims=True)
            acc_scr[c] = alpha * acc_scr[c] + jnp.dot(pr.astype(BF16), v, preferred_element_type=F32)
            m_scr[c] = m_new

    on_diag = (ki + 1) * tk > qi * tq + 1

    @pl.when(on_diag)
    def _():
        step(True)

    @pl.when(jnp.logical_not(on_diag))
    def _():
        step(False)

    @pl.when(ki == (qi + 1) * kpq - 1)
    def _():
        lam = _diff_lambda(lq1, lk1, lq2, lk2, lam_init)
        o = acc_scr[0] / l_scr[0] - lam * (acc_scr[1] / l_scr[1])
        o_ref[...] = (_rms(o) * sg_ref[...] * (1.0 - lam_init)).astype(BF16)


def _attn_prompt(q, k, v, lam_vecs, sub_gain, layer, batch, seq):
    t = min(ATTN_TILE, seq)
    nq = seq // t
    pairs = [(a, b) for a in range(nq) for b in range(a + 1)]
    qi_tab = jnp.asarray([a for a, _ in pairs], jnp.int32)
    ki_tab = jnp.asarray([b for _, b in pairs], jnp.int32)
    vec = lambda a: pl.BlockSpec(a.shape, lambda b, h, p, qt, kt: (0, 0))
    kern = functools.partial(_attn_prompt_kernel, lam_init=_lam_init(layer), tq=t, tk=t)
    return pl.pallas_call(
        kern,
        grid_spec=pltpu.PrefetchScalarGridSpec(
            num_scalar_prefetch=2,
            grid=(batch, H_A, len(pairs)),
            in_specs=[
                pl.BlockSpec((t, LANES), lambda b, h, p, qt, kt: (b * nq + qt[p], h)),
                pl.BlockSpec((t, LANES), lambda b, h, p, qt, kt: (b * nq + kt[p], h)),
                pl.BlockSpec((t, LANES), lambda b, h, p, qt, kt: (b * nq + kt[p], h)),
            ] + [vec(a) for a in lam_vecs] + [vec(sub_gain)],
            out_specs=pl.BlockSpec((t, LANES), lambda b, h, p, qt, kt: (b * nq + qt[p], h)),
            scratch_shapes=[pltpu.VMEM((2, t, 1), F32), pltpu.VMEM((2, t, 1), F32),
                            pltpu.VMEM((2, t, DV_A), F32)],
        ),
        out_shape=jax.ShapeDtypeStruct((batch * seq, A_V), BF16),
        compiler_params=_params(("arbitrary", "arbitrary", "arbitrary")),
        name="attn_prompt",
    )(qi_tab, ki_tab, q, k, v, *lam_vecs, sub_gain)


def _attn_sample_kernel(pt_ref, q_ref, kn_ref, vn_ref, lq1, lk1, lq2, lk2, sg_ref, *rest,
                        n_pages, n_new, lam_init):
    k_pages = rest[:n_pages]
    v_pages = rest[n_pages:2 * n_pages]
    o_ref = rest[2 * n_pages]
    k_scr, v_scr = rest[2 * n_pages + 1:]
    for j in range(n_pages):
        rows = slice(j * PAGE_SIZE, (j + 1) * PAGE_SIZE)
        k_scr[rows, :] = k_pages[j][...].astype(BF16)
        v_scr[rows, :] = v_pages[j][...].astype(BF16)

    lam = _diff_lambda(lq1, lk1, lq2, lk2, lam_init)
    lane = lax.broadcasted_iota(jnp.int32, (1, LANES), 1)
    r = lax.broadcasted_iota(jnp.int32, (2 * SUBLANES, SUBLANES), 0) % SUBLANES
    c = lax.broadcasted_iota(jnp.int32, (2 * SUBLANES, SUBLANES), 1)
    keep_new = jnp.logical_and(c <= r, c < n_new)
    for h in range(H_A):
        hs = slice(h * LANES, (h + 1) * LANES)
        qh = q_ref[:, hs]
        q2 = jnp.concatenate([jnp.where(lane < DH_A, qh, 0.0), jnp.where(lane >= DH_A, qh, 0.0)], axis=0)
        s_past = lax.dot_general(q2.astype(BF16), k_scr[:, hs], NT_DIMS, preferred_element_type=F32)
        s_new = lax.dot_general(q2, kn_ref[:, hs], NT_DIMS, preferred_element_type=F32)
        s_new = jnp.where(keep_new, s_new, -jnp.inf)
        m = jnp.maximum(jnp.max(s_past, axis=-1, keepdims=True), jnp.max(s_new, axis=-1, keepdims=True))
        p_past = jnp.exp(s_past - m)
        p_new = jnp.exp(s_new - m)
        inv = 1.0 / (jnp.sum(p_past, axis=-1, keepdims=True) + jnp.sum(p_new, axis=-1, keepdims=True))
        p_past = p_past * inv
        p_new = p_new * inv
        a_past = p_past[:SUBLANES] - lam * p_past[SUBLANES:]
        a_new = p_new[:SUBLANES] - lam * p_new[SUBLANES:]
        o = (jnp.dot(a_past.astype(BF16), v_scr[:, hs], preferred_element_type=F32)
             + jnp.dot(a_new, vn_ref[:, hs], preferred_element_type=F32))
        o_ref[:, hs] = _rms(o) * sg_ref[...] * (1.0 - lam_init)


def _attn_sample(q8, kn8, vn8, cache_k, cache_v, page_table, lam_vecs, sub_gain, layer, n_new):
    bd, n_pages = page_table.shape
    past = n_pages * PAGE_SIZE
    seq_spec = pl.BlockSpec((None, SUBLANES, A_QK), lambda b, pt: (b, 0, 0))
    vec = lambda a: pl.BlockSpec(a.shape, lambda b, pt: (0, 0))

    def page_spec(j):
        return pl.BlockSpec((None, None, PAGE_SIZE, A_QK), lambda b, pt: (layer, pt[b, j], 0, 0))

    kern = functools.partial(_attn_sample_kernel, n_pages=n_pages, n_new=n_new, lam_init=_lam_init(layer))
    return pl.pallas_call(
        kern,
        grid_spec=pltpu.PrefetchScalarGridSpec(
            num_scalar_prefetch=1,
            grid=(bd,),
            in_specs=[seq_spec, seq_spec, seq_spec] + [vec(a) for a in lam_vecs] + [vec(sub_gain)]
                     + [page_spec(j) for j in range(n_pages)] * 2,
            out_specs=seq_spec,
            scratch_shapes=[pltpu.VMEM((past, A_QK), BF16), pltpu.VMEM((past, A_V), BF16)],
        ),
        out_shape=jax.ShapeDtypeStruct((bd, SUBLANES, A_V), F32),
        compiler_params=_params(("arbitrary",)),
        name="attn_sample",
    )(page_table, q8, kn8, vn8, *lam_vecs, sub_gain, *([cache_k] * n_pages), *([cache_v] * n_pages))


def _ret_prompt_kernel(q_ref, k_ref, v_ref, g_ref, o_ref, st_ref, m_scr):
    ci = pl.program_id(1)
    C = RET_CHUNK

    @pl.when(ci == 0)
    def _():
        m_scr[...] = jnp.zeros(m_scr.shape, F32)

    lane = lax.broadcasted_iota(jnp.int32, (1, LANES), 1)
    i = lax.broadcasted_iota(jnp.int32, (C, C), 0)
    j = lax.broadcasted_iota(jnp.int32, (C, C), 1)
    rel = (i - j).astype(F32)
    pos = lax.broadcasted_iota(jnp.int32, (C, 1), 0).astype(F32)
    for h in range(H_B):
        lg = _log_gamma(h)
        pair = slice((h // 2) * LANES, (h // 2 + 1) * LANES)
        hs = slice(h * DV_B, (h + 1) * DV_B)
        in_head = (lane // DK_B) == (h % 2)
        q2 = q_ref[:, pair]
        k2 = k_ref[:, pair]
        vh = v_ref[:, hs]
        qm = jnp.where(in_head, q2, jnp.zeros_like(q2))
        decay = jnp.where(rel >= 0, jnp.exp(lg * jnp.maximum(rel, 0.0)), 0.0)
        s = lax.dot_general(qm, k2, NT_DIMS, preferred_element_type=F32) * decay
        q_fac = jnp.exp((pos + 1.0) * lg)
        k_fac = jnp.exp((C - 1.0 - pos) * lg)
        qf = (qm.astype(F32) * q_fac).astype(BF16)
        mh = m_scr[h]
        o = (jnp.dot(s.astype(BF16), vh, preferred_element_type=F32)
             + jnp.dot(qf, mh.astype(BF16), preferred_element_type=F32))
        kf = (k2.astype(F32) * k_fac).T.astype(BF16)
        m_new = math.exp(C * lg) * mh + jnp.dot(kf, vh, preferred_element_type=F32)
        m_scr[h] = m_new
        o_ref[:, hs] = (_rms(o) * g_ref[:, hs].astype(F32)).astype(BF16)
        rows = slice((h % 2) * DK_B, (h % 2 + 1) * DK_B)
        st_ref[h] = m_new[rows, :]


def _ret_prompt(qb, kb, vb, g_ret, batch, seq):
    nc = seq // RET_CHUNK
    row = lambda w: pl.BlockSpec((RET_CHUNK, w), lambda b, c: (b * nc + c, 0))
    return pl.pallas_call(
        _ret_prompt_kernel,
        grid=(batch, nc),
        in_specs=[row(B_QK), row(B_QK), row(B_V), row(B_V)],
        out_specs=[row(B_V), pl.BlockSpec((None, H_B, DK_B, DV_B), lambda b, c: (b, 0, 0, 0))],
        out_shape=[jax.ShapeDtypeStruct((batch * seq, B_V), BF16),
                   jax.ShapeDtypeStruct((batch, H_B, DK_B, DV_B), F32)],
        scratch_shapes=[pltpu.VMEM((H_B, LANES, DV_B), F32)],
        compiler_params=_params(("arbitrary", "arbitrary")),
        name="ret_prompt",
    )(qb, kb, vb, g_ret)


def _ret_sample_kernel(q_ref, k_ref, kt_ref, v_ref, g_ref, st_ref, o_ref, ns_ref, *, n_new):
    T = SUBLANES
    i = lax.broadcasted_iota(jnp.int32, (T, T), 0)
    j = lax.broadcasted_iota(jnp.int32, (T, T), 1)
    rel = (i - j).astype(F32)
    pos = lax.broadcasted_iota(jnp.int32, (T, 1), 0).astype(F32)
    posl = lax.broadcasted_iota(jnp.int32, (1, T), 1).astype(F32)
    for h in range(H_B):
        lg = _log_gamma(h)
        ks = slice(h * DK_B, (h + 1) * DK_B)
        hs = slice(h * DV_B, (h + 1) * DV_B)
        q = q_ref[:, ks]
        k = k_ref[:, ks]
        v = v_ref[:, hs]
        st = st_ref[h]
        decay = jnp.where(rel >= 0, jnp.exp(lg * jnp.maximum(rel, 0.0)), 0.0)
        s = lax.dot_general(q, k, NT_DIMS, preferred_element_type=F32) * decay
        q_fac = jnp.exp((pos + 1.0) * lg)
        o = (jnp.dot(s, v, preferred_element_type=F32)
             + jnp.dot(q * q_fac, st, preferred_element_type=F32))
        k_fac = jnp.exp((n_new - 1.0 - posl) * lg)
        kt = kt_ref[h] * k_fac
        new = math.exp(n_new * lg) * st
        for t in range(n_new):
            new = new + kt[:, t:t + 1] * v[t:t + 1, :]
        ns_ref[h] = new
        o_ref[:, hs] = _rms(o) * g_ref[:, hs]


def _ret_sample(q8, k8, kt8, v8, g8, state, n_new):
    bd = state.shape[0]
    seq = lambda w: pl.BlockSpec((None, SUBLANES, w), lambda b: (b, 0, 0))
    st = pl.BlockSpec((None, H_B, DK_B, DV_B), lambda b: (b, 0, 0, 0))
    return pl.pallas_call(
        functools.partial(_ret_sample_kernel, n_new=n_new),
        grid=(bd,),
        in_specs=[seq(B_QK), seq(B_QK), pl.BlockSpec((None, H_B, DK_B, SUBLANES), lambda b: (b, 0, 0, 0)),
                  seq(B_V), seq(B_V), st],
        out_specs=[seq(B_V), st],
        out_shape=[jax.ShapeDtypeStruct((bd, SUBLANES, B_V), F32),
                   jax.ShapeDtypeStruct((bd, H_B, DK_B, DV_B), F32)],
        compiler_params=_params(("arbitrary",)),
        name="ret_sample",
    )(q8, k8, kt8, v8, g8, state)


def _post_mix_kernel(*refs, routed):
    (oa_ref, ob_ref, ga_ref, gb_ref, x_ref, g1_ref, sc_ref, sh_ref, gain_ref,
     woa_ref, wor_ref, wout_ref) = refs[:12]
    if routed:
        wr_ref, br_ref, x_out, h_out, route_out, cnt_out, cnt_scr = refs[12:]
    else:
        x_out, h_out = refs[12:]
    out_a = jnp.dot(oa_ref[...], woa_ref[...], preferred_element_type=F32)
    out_b = jnp.dot(ob_ref[...], wor_ref[...], preferred_element_type=F32)
    mix = (ga_ref[...].astype(F32) * out_a + gb_ref[...].astype(F32) * out_b).astype(BF16)
    x = x_ref[...] + g1_ref[...] * jnp.dot(mix, wout_ref[...], preferred_element_type=F32)
    x_out[...] = x
    h2 = _rms(x) * gain_ref[...] * (1.0 + sc_ref[...]) + sh_ref[...]
    if not routed:
        h_out[...] = h2.astype(BF16)
        return
    h_out[...] = h2
    tm = h2.shape[0]

    @pl.when(pl.program_id(0) == 0)
    def _():
        cnt_scr[...] = jnp.zeros(cnt_scr.shape, F32)

    lane = lax.broadcasted_iota(jnp.int32, (tm, LANES), 1).astype(F32)
    logits = jnp.dot(h2.astype(BF16), wr_ref[...], preferred_element_type=F32) + br_ref[...]
    logits = jnp.where(lane < N_EXPERTS, logits, -jnp.inf)
    v1 = jnp.max(logits, axis=-1, keepdims=True)
    i1 = jnp.min(jnp.where(logits == v1, lane, float(LANES)), axis=-1, keepdims=True)
    rest = jnp.where(lane == i1, -jnp.inf, logits)
    v2 = jnp.max(rest, axis=-1, keepdims=True)
    i2 = jnp.min(jnp.where(rest == v2, lane, float(LANES)), axis=-1, keepdims=True)
    e2 = jnp.exp(v2 - v1)
    w1 = 1.0 / (1.0 + e2)
    w2 = e2 / (1.0 + e2)
    hot1 = lane == i1
    hot2 = lane == i2
    hot = jnp.logical_or(hot1, hot2).astype(BF16)
    tr = lax.broadcasted_iota(jnp.int32, (tm, tm), 0)
    tc = lax.broadcasted_iota(jnp.int32, (tm, tm), 1)
    before = (tc < tr).astype(BF16)
    ranks = jnp.dot(before, hot, preferred_element_type=F32) + cnt_scr[...]
    r1 = jnp.sum(jnp.where(hot1, ranks, 0.0), axis=-1, keepdims=True)
    r2 = jnp.sum(jnp.where(hot2, ranks, 0.0), axis=-1, keepdims=True)
    cnt = cnt_scr[...] + jnp.sum(hot.astype(F32), axis=0, keepdims=True)
    cnt_scr[...] = cnt
    cnt_out[...] = cnt
    cols = (i1, i2, w1, w2, r1, r2)
    packed = jnp.zeros((tm, LANES), F32)
    for n, col in enumerate(cols):
        packed = jnp.where(lane == float(n), col, packed)
    route_out[...] = packed


def _post_mix(oa, ob, ga, gb, x, mods, gain, woa, wor, wout, tm, router=None):
    n, d = x.shape
    g1, sh2, sc2 = mods["arr"][2], mods["arr"][3], mods["arr"][4]
    idx = mods["idx"](tm)
    row = lambda w: pl.BlockSpec((tm, w), lambda i: (i, 0))
    const = lambda a: pl.BlockSpec(a.shape, lambda i: (0,) * a.ndim)
    routed = router is not None
    ins = [oa, ob, ga, gb, x, g1, sc2, sh2, gain, woa, wor, wout]
    in_specs = [row(A_V), row(B_V), row(d), row(d), row(d), _mod_spec(g1, idx), _mod_spec(sc2, idx),
                _mod_spec(sh2, idx), const(gain), const(woa), const(wor), const(wout)]
    out_specs = [row(d), row(d)]
    out_shape = [jax.ShapeDtypeStruct((n, d), F32), jax.ShapeDtypeStruct((n, d), F32 if routed else BF16)]
    scratch = []
    if routed:
        ins += list(router)
        in_specs += [const(a) for a in router]
        out_specs += [row(LANES), pl.BlockSpec((1, LANES), lambda i: (0, 0))]
        out_shape += [jax.ShapeDtypeStruct((n, LANES), F32), jax.ShapeDtypeStruct((1, LANES), F32)]
        scratch = [pltpu.VMEM((1, LANES), F32)]
    return pl.pallas_call(
        functools.partial(_post_mix_kernel, routed=routed),
        grid=(n // tm,),
        in_specs=in_specs,
        out_specs=out_specs,
        out_shape=out_shape,
        scratch_shapes=scratch,
        compiler_params=_params(("arbitrary",)),
        name="post_mix_routed" if routed else "post_mix",
    )(*ins)


def _ff_chunks(f):
    step = 512
    return [(c, min(step, f - c)) for c in range(0, f, step)]


def _swiglu_kernel(te_ref, tv_ref, h_ref, wg_ref, wu_ref, wd_ref, *rest, residual):
    if residual:
        x_ref, g2_ref, o_ref = rest
    else:
        (o_ref,) = rest

    @pl.when(tv_ref[pl.program_id(0)] > 0)
    def _():
        h = h_ref[...].astype(BF16)
        acc = None
        for c0, cw in _ff_chunks(wg_ref.shape[1]):
            g = jnp.dot(h, wg_ref[:, c0:c0 + cw], preferred_element_type=F32)
            u = jnp.dot(h, wu_ref[:, c0:c0 + cw], preferred_element_type=F32)
            a = (_silu(g) * u).astype(BF16)
            part = jnp.dot(a, wd_ref[c0:c0 + cw, :], preferred_element_type=F32)
            acc = part if acc is None else acc + part
        if residual:
            o_ref[...] = x_ref[...] + g2_ref[...] * acc
        else:
            o_ref[...] = acc

    @pl.when(tv_ref[pl.program_id(0)] == 0)
    def _():
        o_ref[...] = jnp.zeros(o_ref.shape, o_ref.dtype)


def _swiglu(h, tile_expert, tile_valid, wg, wu, wd, tm, residual=None):
    n, d = h.shape
    f = wg.shape[-1]
    row = pl.BlockSpec((tm, d), lambda i, te, tv: (i, 0))
    ins = [h, wg, wu, wd]
    in_specs = [row,
                pl.BlockSpec((None, d, f), lambda i, te, tv: (te[i], 0, 0)),
                pl.BlockSpec((None, d, f), lambda i, te, tv: (te[i], 0, 0)),
                pl.BlockSpec((None, f, d), lambda i, te, tv: (te[i], 0, 0))]
    if residual is not None:
        x, mods = residual
        g2 = mods["arr"][5]
        idx = mods["idx"](tm)
        ins += [x, g2]
        in_specs += [row, pl.BlockSpec((None,) + g2.shape[1:], lambda i, te, tv: idx(i))]
    return pl.pallas_call(
        functools.partial(_swiglu_kernel, residual=residual is not None),
        grid_spec=pltpu.PrefetchScalarGridSpec(
            num_scalar_prefetch=2,
            grid=(n // tm,),
            in_specs=in_specs,
            out_specs=row,
        ),
        out_shape=jax.ShapeDtypeStruct((n, d), F32),
        compiler_params=_params(("arbitrary",)),
        name="swiglu_dense" if residual is not None else "swiglu_experts",
    )(tile_expert, tile_valid, *ins)


def _dispatch_kernel(pos_ref, h_hbm, dst_in, dst_hbm, sem, *, tb):
    del dst_in
    base = pl.program_id(0) * tb

    def copy(r, slot):
        return pltpu.make_async_copy(h_hbm.at[pl.ds(base + r, 1)],
                                     dst_hbm.at[pl.ds(pos_ref[2 * r + slot], 1)], sem)

    def issue(r, carry):
        copy(r, 0).start()
        copy(r, 1).start()
        return carry

    def drain(r, carry):
        copy(r, 0).wait()
        copy(r, 1).wait()
        return carry

    lax.fori_loop(0, tb, issue, 0)
    lax.fori_loop(0, tb, drain, 0)


def _dispatch(h2, pos, dst, tb):
    n, d = h2.shape
    return pl.pallas_call(
        functools.partial(_dispatch_kernel, tb=tb),
        grid=(n // tb,),
        in_specs=[pl.BlockSpec((2 * tb,), lambda i: (i,), memory_space=pltpu.SMEM),
                  pl.BlockSpec(memory_space=pl.ANY),
                  pl.BlockSpec(memory_space=pl.ANY)],
        out_specs=pl.BlockSpec(memory_space=pl.ANY),
        out_shape=jax.ShapeDtypeStruct(dst.shape, dst.dtype),
        scratch_shapes=[pltpu.SemaphoreType.DMA(())],
        input_output_aliases={2: 0},
        compiler_params=_params(("arbitrary",)),
        name="dispatch",
    )(pos, h2, dst)


def _combine_kernel(pos_ref, ye_hbm, route_ref, x_ref, g2_ref, o_ref, buf, sem, *, tb):
    def copy(r, slot):
        return pltpu.make_async_copy(ye_hbm.at[pl.ds(pos_ref[2 * r + slot], 1)],
                                     buf.at[slot, pl.ds(r, 1)], sem)

    def issue(r, carry):
        copy(r, 0).start()
        copy(r, 1).start()
        return carry

    def drain(r, carry):
        copy(r, 0).wait()
        copy(r, 1).wait()
        return carry

    lax.fori_loop(0, tb, issue, 0)
    lax.fori_loop(0, tb, drain, 0)
    route = route_ref[...]
    w1 = route[:, 2:3]
    w2 = route[:, 3:4]
    o_ref[...] = x_ref[...] + g2_ref[...] * (w1 * buf[0] + w2 * buf[1])


def _combine(ye, pos, route, x, mods, tb):
    n, d = x.shape
    g2 = mods["arr"][5]
    idx = mods["idx"](tb)
    row = lambda w: pl.BlockSpec((tb, w), lambda i: (i, 0))
    return pl.pallas_call(
        functools.partial(_combine_kernel, tb=tb),
        grid=(n // tb,),
        in_specs=[pl.BlockSpec((2 * tb,), lambda i: (i,), memory_space=pltpu.SMEM),
                  pl.BlockSpec(memory_space=pl.ANY),
                  row(LANES), row(d), _mod_spec(g2, idx)],
        out_specs=row(d),
        out_shape=jax.ShapeDtypeStruct((n, d), F32),
        scratch_shapes=[pltpu.VMEM((2, tb, d), F32), pltpu.SemaphoreType.DMA(())],
        compiler_params=_params(("arbitrary",)),
        name="combine",
    )(pos, ye, route, x, g2)


def _rope_tables(pos):
    inv = ROPE_THETA ** (-jnp.arange(0, DH_A, 2, dtype=F32) / DH_A)
    ang = pos.astype(F32)[:, None] * inv[None, :]
    ang = jnp.concatenate([ang, ang], axis=-1)
    sign = jnp.where(jnp.arange(DH_A) < DH_A // 2, -1.0, 1.0).astype(F32)
    cos = jnp.tile(jnp.cos(ang), (1, LANES // DH_A))
    sin = jnp.tile(jnp.sin(ang) * sign[None, :], (1, LANES // DH_A))
    return cos, sin


def _pad_rows(a, bd, t):
    a = a.astype(F32).reshape(bd, t, a.shape[-1])
    return jnp.pad(a, ((0, 0), (0, SUBLANES - t), (0, 0)))


def _group_mods(m, rows_per_group, seq):
    g = m.shape[0]
    parts = jnp.split(m, N_MOD, axis=-1)
    if rows_per_group is None:
        arr = [p.reshape(g, 1, D_MODEL) for p in parts]
        return {"arr": arr, "idx": lambda tm: (lambda i: (i * tm // seq, 0, 0))}
    n = g * rows_per_group

    def expand(p, tm):
        return jnp.repeat(p, rows_per_group, axis=0).reshape(n // tm, tm, D_MODEL)

    tm = min(ROW_TILE, n)
    return {"arr": [expand(p, tm) for p in parts], "idx": lambda tm_: (lambda i: (i, 0, 0))}


def kernel(x_prompt, x_sample, cache_k, cache_v, state_ret, page_table, c_prompt, c_sample, w_mod, b_mod, norm_mix, norm_ffn, w_in, q_norm, k_norm, lambda_q1, lambda_k1, lambda_q2, lambda_k2, sub_norm, w_o_attn, w_o_ret, w_out, ffn_w_gate, ffn_w_up, ffn_w_down, w_router, b_router, moe_w_gate, moe_w_up, moe_w_down):
    depth = w_in.shape[0]
    bp, seq, d = x_prompt.shape
    bd, t_new, _ = x_sample.shape
    n_pages = page_table.shape[1]
    past = n_pages * PAGE_SIZE
    n_p = bp * seq
    n_s = bd * t_new
    tm_p = min(ROW_TILE, n_p)
    tm_s = min(ROW_TILE, n_s)

    n_c = bp + bd
    n_c_pad = -(-n_c // SUBLANES) * SUBLANES
    c_all = jnp.pad(jnp.concatenate([c_prompt, c_sample], axis=0), ((0, n_c_pad - n_c), (0, 0)))
    mod_all = _modulation(c_all, w_mod, b_mod)

    cos_p, sin_p = _rope_tables(jnp.arange(seq))
    cos_s, sin_s = _rope_tables(jnp.tile(past + jnp.arange(t_new), bd))
    npool = cache_k.shape[1]
    cache_k2 = cache_k.reshape(depth, npool, PAGE_SIZE, A_QK)
    cache_v2 = cache_v.reshape(depth, npool, PAGE_SIZE, A_V)

    n_assign = 2 * (n_p + n_s)
    n_tiles = -(-(n_assign + N_EXPERTS * (EXPERT_TILE - 1)) // EXPERT_TILE)

    xp = x_prompt.reshape(n_p, d)
    xs = x_sample.reshape(n_s, d)
    kp_l, vp_l, sp_l, ks_l, vs_l, ss_l = [], [], [], [], [], []
    for l in range(depth):
        w_in16 = w_in[l].astype(BF16)
        woa, wor, wout = w_o_attn[l].astype(BF16), w_o_ret[l].astype(BF16), w_out[l].astype(BF16)
        tile2 = lambda a: jnp.tile(a.reshape(1, -1), (1, LANES // a.shape[-1]))
        qg, kg = tile2(q_norm[l]), tile2(k_norm[l])
        lam_vecs = [a[l].reshape(1, DH_A) for a in (lambda_q1, lambda_k1, lambda_q2, lambda_k2)]
        sub_gain = sub_norm[l].reshape(1, DV_A)
        gain_mix = norm_mix[l].reshape(1, d)
        gain_ffn = norm_ffn[l].reshape(1, d)
        mods_p = _group_mods(mod_all[l, :bp], None, seq)
        mods_s = _group_mods(mod_all[l, bp:bp + bd], t_new, None)

        (q, k32, k16, v32, v16, qb, kb, vb, g_ret, g_a, g_b) = _proj_in(
            xp, mods_p, gain_mix, w_in16, qg, kg, cos_p, sin_p, seq // tm_p, tm_p)
        oa = _attn_prompt(q, k16, v16, lam_vecs, sub_gain, l, bp, seq)
        ob, st_p = _ret_prompt(qb, kb, vb, g_ret, bp, seq)
        kp_l.append(k32.reshape(bp, seq, H_A, 2, DH_A))
        vp_l.append(v32.reshape(bp, seq, H_A, DV_A))
        sp_l.append(st_p)

        (q_s, k32_s, _, v32_s, _, qb_s, kb_s, vb_s, g_ret_s, g_a_s, g_b_s) = _proj_in(
            xs, mods_s, gain_mix, w_in16, qg, kg, cos_s, sin_s, n_s // tm_s, tm_s)
        oa_s = _attn_sample(_pad_rows(q_s, bd, t_new), _pad_rows(k32_s, bd, t_new), _pad_rows(v32_s, bd, t_new),
                            cache_k2, cache_v2, page_table, lam_vecs, sub_gain, l, t_new)
        oa_s = oa_s[:, :t_new].reshape(n_s, A_V).astype(BF16)
        kb8 = _pad_rows(kb_s, bd, t_new)
        kt8 = kb8.reshape(bd, SUBLANES, H_B, DK_B).transpose(0, 2, 3, 1)
        ob_s, st_s = _ret_sample(_pad_rows(qb_s, bd, t_new), kb8, kt8, _pad_rows(vb_s, bd, t_new),
                                 _pad_rows(g_ret_s, bd, t_new), state_ret[l], t_new)
        ob_s = ob_s[:, :t_new].reshape(n_s, B_V).astype(BF16)
        ks_l.append(k32_s.reshape(bd, t_new, H_A, 2, DH_A))
        vs_l.append(v32_s.reshape(bd, t_new, H_A, DV_A))
        ss_l.append(st_s)

        j = l // 2
        if l % 2 == 0:
            wg, wu, wd = (a[j:j + 1].astype(BF16) for a in (ffn_w_gate, ffn_w_up, ffn_w_down))
            xp_mid, h2_p = _post_mix(oa, ob, g_a, g_b, xp, mods_p, gain_ffn, woa, wor, wout, tm_p)
            xs_mid, h2_s = _post_mix(oa_s, ob_s, g_a_s, g_b_s, xs, mods_s, gain_ffn, woa, wor, wout, tm_s)
            zeros = lambda n: jnp.zeros((n,), jnp.int32)
            ones = lambda n: jnp.ones((n,), jnp.int32)
            xp = _swiglu(h2_p, zeros(n_p // tm_p), ones(n_p // tm_p), wg, wu, wd, tm_p, residual=(xp_mid, mods_p))
            xs = _swiglu(h2_s, zeros(n_s // tm_s), ones(n_s // tm_s), wg, wu, wd, tm_s, residual=(xs_mid, mods_s))
        else:
            wg, wu, wd = (a[j].astype(BF16) for a in (moe_w_gate, moe_w_up, moe_w_down))
            wr = jnp.pad(w_router[j], ((0, 0), (0, LANES - N_EXPERTS))).astype(BF16)
            br = jnp.pad(b_router[j], (0, LANES - N_EXPERTS)).reshape(1, LANES)
            xp_mid, h2_p, route_p, cnt_p = _post_mix(oa, ob, g_a, g_b, xp, mods_p, gain_ffn, woa, wor, wout,
                                                     tm_p, router=(wr, br))
            xs_mid, h2_s, route_s, cnt_s = _post_mix(oa_s, ob_s, g_a_s, g_b_s, xs, mods_s, gain_ffn, woa, wor,
                                                     wout, tm_s, router=(wr, br))
            cp = cnt_p[0, :N_EXPERTS].astype(jnp.int32)
            cs = cnt_s[0, :N_EXPERTS].astype(jnp.int32)
            padded = -(-(cp + cs) // EXPERT_TILE) * EXPERT_TILE
            ends = jnp.cumsum(padded)
            starts = ends - padded

            def positions(route, offset):
                e1 = route[:, 0].astype(jnp.int32)
                e2 = route[:, 1].astype(jnp.int32)
                p1 = (starts + offset)[e1] + route[:, 4].astype(jnp.int32)
                p2 = (starts + offset)[e2] + route[:, 5].astype(jnp.int32)
                return jnp.stack([p1, p2], axis=-1).reshape(-1)

            pos_p = positions(route_p, 0)
            pos_s = positions(route_s, cp)
            tile_start = jnp.arange(n_tiles, dtype=jnp.int32) * EXPERT_TILE
            tile_expert = jnp.minimum(jnp.sum(tile_start[:, None] >= ends[None, :], axis=-1),
                                      N_EXPERTS - 1).astype(jnp.int32)
            tile_valid = (tile_start < ends[-1]).astype(jnp.int32)
            rows = jnp.zeros((n_tiles * EXPERT_TILE, d), F32)
            rows = _dispatch(h2_p, pos_p, rows, tm_p)
            rows = _dispatch(h2_s, pos_s, rows, tm_s)
            ye = _swiglu(rows, tile_expert, tile_valid, wg, wu, wd, EXPERT_TILE)
            xp = _combine(ye, pos_p, route_p, xp_mid, mods_p, tm_p)
            xs = _combine(ye, pos_s, route_s, xs_mid, mods_s, tm_s)

    return (xp.reshape(bp, seq, d), xs.reshape(bd, t_new, d),
            jnp.stack(kp_l), jnp.stack(vp_l), jnp.stack(sp_l),
            jnp.stack(ks_l), jnp.stack(vs_l), jnp.stack(ss_l))
```
